```python
import math
import jax, jax.numpy as jnp
from jax import lax
import numpy as np

D_MODEL = 2048
BATCH = 8
SEQ = 4096
DEPTH = 4

GRID_W = 64
CTX_LEN = 256
EPS = 1e-6

S5_WIDTH = D_MODEL // 2
S5_GROUP = 16
S5_GROUPS = S5_WIDTH // S5_GROUP
S5_STATE = 64
S5_CHUNK = 128
S5_STEP_MIN = 1e-3
S5_STEP_MAX = 1e-1

RET_HEADS = 8
RET_QK_DIM = (D_MODEL // 2) // RET_HEADS
RET_V_DIM = (D_MODEL // 2) // RET_HEADS
RET_QK = RET_HEADS * RET_QK_DIM
RET_V = RET_HEADS * RET_V_DIM
RET_CHUNK = 128
ROPE_BASE = 10000.0

N_EXPERTS = 16
N_GROUPS = 4
EXPERTS_PER_GROUP = N_EXPERTS // N_GROUPS
TOP_K = 2
D_EXPERT = D_MODEL // 2
MOE_BLOCK = 512

IN_SPLITS = [S5_WIDTH, RET_QK, RET_QK, RET_V, RET_V, D_MODEL, D_MODEL]
IN_COLS = sum(IN_SPLITS)
IN_CUTS = [int(v) for v in np.cumsum(IN_SPLITS)[:-1]]

kernel_name = 'hybrid_s5_retention_grouped_moe_dit'


def rms_norm(x, g):
    xf = x.astype(jnp.float32)
    y = xf * lax.rsqrt(jnp.mean(xf * xf, axis=-1, keepdims=True) + EPS)
    return (y * g.astype(jnp.float32)).astype(x.dtype)


def modulate(h, shift, scale):
    return h * (1.0 + scale) + shift


def s5_discretize(lam_re, lam_im, log_step, b_re, b_im):
    lam = lax.complex(lam_re.astype(jnp.float32), lam_im.astype(jnp.float32))
    delta = jnp.exp(log_step.astype(jnp.float32))[:, None]
    lam_bar = jnp.exp(lam * delta)
    b = lax.complex(b_re.astype(jnp.float32), b_im.astype(jnp.float32))
    b_bar = ((lam_bar - 1.0) / lam)[..., None] * b
    return lam_bar, b_bar


def s5_scan(u, lam_bar, b_bar, cmat, h0):
    bsz, L = u.shape[0], u.shape[1]
    nc = L // S5_CHUNK
    uc = u.reshape(bsz, nc, S5_CHUNK, S5_GROUPS, S5_GROUP).transpose(1, 0, 2, 3, 4)

    def combine(left, right):
        a1, b1 = left
        a2, b2 = right
        return a2 * a1, a2 * b1 + b2

    def step(h, u_blk):
        bu = jnp.einsum('gph,btgh->btgp', b_bar, u_blk.astype(jnp.complex64))
        a = jnp.broadcast_to(lam_bar, bu.shape)
        a_cum, h_loc = lax.associative_scan(combine, (a, bu), axis=1)
        hs = a_cum * h[:, None] + h_loc
        y = jnp.einsum('ghp,btgp->btgh', cmat, hs).real
        return hs[:, -1], y

    h_last, ys = lax.scan(step, h0, uc)
    y = ys.transpose(1, 0, 2, 3, 4).reshape(bsz, L, S5_GROUPS, S5_GROUP)
    return y, h_last


def s5_mixer(u_c, u_l, lam_re, lam_im, log_step, b_re, b_im, c_re, c_im, d_skip, w_glu):
    dt = u_l.dtype
    bsz = u_l.shape[0]
    uc = u_c.astype(jnp.float32).reshape(bsz, u_c.shape[1], S5_GROUPS, S5_GROUP)
    ul = u_l.astype(jnp.float32).reshape(bsz, u_l.shape[1], S5_GROUPS, S5_GROUP)
    d = d_skip.astype(jnp.float32).reshape(S5_GROUPS, S5_GROUP)
    yc = d * uc
    yl = d * ul
    for direction in range(2):
        lam_bar, b_bar = s5_discretize(lam_re[direction], lam_im[direction], log_step[direction],
                                       b_re[direction], b_im[direction])
        cmat = lax.complex(c_re[direction].astype(jnp.float32), c_im[direction].astype(jnp.float32))
        h0 = jnp.zeros((bsz, S5_GROUPS, S5_STATE), jnp.complex64)
        if direction == 0:
            y_c, h_c = s5_scan(uc, lam_bar, b_bar, cmat, h0)
            y_l, _ = s5_scan(ul, lam_bar, b_bar, cmat, h_c)
        else:
            y_c, h_c = s5_scan(jnp.flip(uc, 1), lam_bar, b_bar, cmat, h0)
            y_l, _ = s5_scan(jnp.flip(ul, 1), lam_bar, b_bar, cmat, h_c)
            y_c, y_l = jnp.flip(y_c, 1), jnp.flip(y_l, 1)
        yc = yc + y_c
        yl = yl + y_l

    def glu(y):
        y = jax.nn.gelu(y.reshape(y.shape[0], y.shape[1], S5_WIDTH)).astype(dt)
        return y * jax.nn.sigmoid(y @ w_glu)

    return glu(yc), glu(yl)


def rope_2d(x):
    L = x.shape[1]
    rows = L // GRID_W
    r, cidx = jnp.meshgrid(jnp.arange(rows, dtype=jnp.float32), jnp.arange(GRID_W, dtype=jnp.float32), indexing='ij')
    r, cidx = r.reshape(-1), cidx.reshape(-1)
    half = x.shape[-1] // 2
    nf = half // 2
    freqs = ROPE_BASE ** (-jnp.arange(nf, dtype=jnp.float32) / nf)

    def rot(xh, pos):
        ang = pos[:, None] * freqs
        cos = jnp.cos(ang)[None, :, None, :]
        sin = jnp.sin(ang)[None, :, None, :]
        x1, x2 = xh[..., :nf], xh[..., nf:]
        return jnp.concatenate([x1 * cos - x2 * sin, x1 * sin + x2 * cos], axis=-1)

    return jnp.concatenate([rot(x[..., :half], r), rot(x[..., half:], cidx)], axis=-1)


def retention_scan(q, k, v, log_gamma, s0):
    bsz, L = q.shape[0], q.shape[1]
    C = RET_CHUNK
    nc = L // C

    def chunks(t):
        return t.reshape(bsz, nc, C, RET_HEADS, t.shape[-1]).transpose(1, 0, 3, 2, 4)

    idx = jnp.arange(C, dtype=jnp.float32)
    diff = idx[:, None] - idx[None, :]
    dmask = jnp.where(diff >= 0, jnp.exp(log_gamma[:, None, None] * jnp.maximum(diff, 0.0)), 0.0)
    xi = jnp.exp(log_gamma[:, None] * (idx + 1.0))[None, :, :, None]
    zeta = jnp.exp(log_gamma[:, None] * (C - 1.0 - idx))[None, :, :, None]
    chunk_decay = jnp.exp(log_gamma * C)[None, :, None, None]

    def step(s, blk):
        qb, kb, vb = blk
        scores = jnp.einsum('bhid,bhjd->bhij', qb, kb) * dmask
        o = jnp.einsum('bhij,bhjv->bhiv', scores, vb) + jnp.einsum('bhid,bhdv->bhiv', qb, s) * xi
        s_new = s * chunk_decay + jnp.einsum('bhjd,bhjv->bhdv', kb * zeta, vb)
        return s_new, o

    s_last, os_ = lax.scan(step, s0, (chunks(q), chunks(k), chunks(v)))
    o = os_.transpose(1, 0, 3, 2, 4).reshape(bsz, L, RET_HEADS, v.shape[-1])
    return o, s_last


def retention_mixer(q_c, k_c, v_c, g_c, q_l, k_l, v_l, g_l, decay_logit):
    dt = q_l.dtype
    bsz = q_l.shape[0]

    def heads(t):
        return t.astype(jnp.float32).reshape(bsz, t.shape[1], RET_HEADS, -1)

    k_scale = RET_QK_DIM ** -0.5
    qc, kc, vc = heads(q_c), heads(k_c) * k_scale, heads(v_c)
    ql, kl, vl = rope_2d(heads(q_l)), rope_2d(heads(k_l)) * k_scale, heads(v_l)
    log_gamma = jax.nn.log_sigmoid(decay_logit.astype(jnp.float32))
    s0 = jnp.zeros((bsz, RET_HEADS, RET_QK_DIM, RET_V_DIM), jnp.float32)
    oc = jnp.zeros(vc.shape, jnp.float32)
    ol = jnp.zeros(vl.shape, jnp.float32)
    for direction in range(2):
        if direction == 0:
            o_c, s_c = retention_scan(qc, kc, vc, log_gamma[0], s0)
            o_l, _ = retention_scan(ql, kl, vl, log_gamma[0], s_c)
        else:
            o_c, s_c = retention_scan(jnp.flip(qc, 1), jnp.flip(kc, 1), jnp.flip(vc, 1), log_gamma[1], s0)
            o_l, _ = retention_scan(jnp.flip(ql, 1), jnp.flip(kl, 1), jnp.flip(vl, 1), log_gamma[1], s_c)
            o_c, o_l = jnp.flip(o_c, 1), jnp.flip(o_l, 1)
        oc = oc + o_c
        ol = ol + o_l

    def finish(o, g):
        o = o * lax.rsqrt(jnp.mean(o * o, axis=-1, keepdims=True) + EPS)
        o = o.reshape(bsz, o.shape[1], RET_V).astype(dt)
        return o * jax.nn.silu(g)

    return finish(oc, g_c), finish(ol, g_l)


def moe_ffn(xf, w_router, router_bias, w_gate, w_up, w_down):
    T, D = xf.shape
    scores = jax.nn.sigmoid(xf.astype(jnp.float32) @ w_router.astype(jnp.float32))
    sel = (scores + router_bias.astype(jnp.float32)).reshape(T, N_GROUPS, EXPERTS_PER_GROUP)
    group_score = lax.top_k(sel, TOP_K)[0].sum(-1)
    g_idx = jnp.argmax(group_score, axis=-1)
    in_group = jnp.take_along_axis(sel, g_idx[:, None, None], axis=1)[:, 0]
    _, local = lax.top_k(in_group, TOP_K)
    expert = g_idx[:, None] * EXPERTS_PER_GROUP + local
    w = jnp.take_along_axis(scores, expert, axis=1)
    w = w / jnp.sum(w, axis=-1, keepdims=True)

    A = T * TOP_K
    flat_e = expert.reshape(A)
    flat_t = jnp.repeat(jnp.arange(T, dtype=jnp.int32), TOP_K)
    flat_w = w.reshape(A)
    order = jnp.argsort(flat_e)
    se, st, sw = flat_e[order], flat_t[order], flat_w[order]
    counts = jnp.bincount(flat_e, length=N_EXPERTS)
    starts = jnp.cumsum(counts) - counts
    padded = (counts + MOE_BLOCK - 1) // MOE_BLOCK * MOE_BLOCK
    pends = jnp.cumsum(padded)
    pstarts = pends - padded
    dest = pstarts[se] + (jnp.arange(A) - starts[se])
    n_blocks = -(-A // MOE_BLOCK) + N_EXPERTS
    P = n_blocks * MOE_BLOCK
    tok = jnp.full((P,), T, jnp.int32).at[dest].set(st)
    wt = jnp.zeros((P,), jnp.float32).at[dest].set(sw)
    block_e = jnp.minimum(jnp.searchsorted(pends, jnp.arange(n_blocks) * MOE_BLOCK, side='right'), N_EXPERTS - 1)
    x_pad = jnp.concatenate([xf, jnp.zeros((1, D), xf.dtype)], axis=0)

    def body(out, blk):
        idx, wb, e = blk
        xb = x_pad[idx]
        h = jax.nn.silu(xb @ w_gate[e]) * (xb @ w_up[e])
        y = (h @ w_down[e]).astype(jnp.float32) * wb[:, None]
        return out.at[idx].add(y), None

    out, _ = lax.scan(body, jnp.zeros((T + 1, D), jnp.float32),
                      (tok.reshape(n_blocks, MOE_BLOCK), wt.reshape(n_blocks, MOE_BLOCK), block_e))
    return out[:T].astype(xf.dtype)


def setup_inputs(seed: int = 0) -> dict:
    key = jax.random.key(seed)
    ks = jax.random.split(key, 32)
    f32 = jnp.float32
    D, E, F = D_MODEL, N_EXPERTS, D_EXPERT
    G, P, H = S5_GROUPS, S5_STATE, S5_GROUP
    nrm = lambda k, shape, s: jax.random.normal(k, shape, f32) * s
    lam_im0 = jnp.broadcast_to(jnp.pi * jnp.arange(P, dtype=f32), (DEPTH, 2, G, P))
    ret_base = jnp.asarray(np.log(2.0 ** (5.0 + np.arange(RET_HEADS)) - 1.0), f32)
    return {
        'x': nrm(ks[0], (BATCH, SEQ, D), 1.0),
        'c': nrm(ks[1], (BATCH, D), 1.0),
        'ctx': nrm(ks[2], (BATCH, CTX_LEN, D), 1.0),
        'c_ctx': nrm(ks[3], (D,), 1.0),
        'ada_w': nrm(ks[4], (DEPTH, D, 6 * D), 0.5 * D ** -0.5),
        'ada_b': nrm(ks[5], (DEPTH, 6 * D), 0.01),
        'norm_mix_g': 1.0 + nrm(ks[6], (DEPTH, D), 0.01),
        'norm_ffn_g': 1.0 + nrm(ks[7], (DEPTH, D), 0.01),
        'final_norm_g': 1.0 + nrm(ks[8], (D,), 0.01),
        'w_in': nrm(ks[9], (DEPTH, D, IN_COLS), D ** -0.5),
        's5_lam_re': -0.5 + nrm(ks[10], (DEPTH, 2, G, P), 0.01),
        's5_lam_im': lam_im0 + nrm(ks[11], (DEPTH, 2, G, P), 0.01),
        's5_log_step': jax.random.uniform(ks[12], (DEPTH, 2, G), f32, math.log(S5_STEP_MIN), math.log(S5_STEP_MAX)),
        's5_b_re': nrm(ks[13], (DEPTH, 2, G, P, H), (2 * H) ** -0.5),
        's5_b_im': nrm(ks[14], (DEPTH, 2, G, P, H), (2 * H) ** -0.5),
        's5_c_re': nrm(ks[15], (DEPTH, 2, G, H, P), (2 * P) ** -0.5),
        's5_c_im': nrm(ks[16], (DEPTH, 2, G, H, P), (2 * P) ** -0.5),
        's5_d': nrm(ks[17], (DEPTH, S5_WIDTH), 1.0),
        's5_glu_w': nrm(ks[18], (DEPTH, S5_WIDTH, S5_WIDTH), S5_WIDTH ** -0.5),
        'ret_decay_logit': ret_base + nrm(ks[19], (DEPTH, 2, RET_HEADS), 0.01),
        'w_branch_s5': nrm(ks[20], (DEPTH, S5_WIDTH, D), S5_WIDTH ** -0.5),
        'w_branch_ret': nrm(ks[21], (DEPTH, RET_V, D), RET_V ** -0.5),
        'w_out': nrm(ks[22], (DEPTH, D, D), D ** -0.5),
        'w_router': nrm(ks[23], (D, E), D ** -0.5),
        'router_bias': nrm(ks[24], (E,), 0.01),
        'moe_w_gate': nrm(ks[25], (DEPTH, E, D, F), D ** -0.5),
        'moe_w_up': nrm(ks[26], (DEPTH, E, D, F), D ** -0.5),
        'moe_w_down': nrm(ks[27], (DEPTH, E, F, D), F ** -0.5),
    }


def reference(x, c, ctx, c_ctx, ada_w, ada_b, norm_mix_g, norm_ffn_g, final_norm_g, w_in,
              s5_lam_re, s5_lam_im, s5_log_step, s5_b_re, s5_b_im, s5_c_re, s5_c_im, s5_d, s5_glu_w,
              ret_decay_logit, w_branch_s5, w_branch_ret, w_out, w_router, router_bias,
              moe_w_gate, moe_w_up, moe_w_down):
    xl, xc = x, ctx
    bsz, n_lat, D = xl.shape
    silu_c = jax.nn.silu(c)
    silu_cc = jax.nn.silu(c_ctx)
    for i in range(DEPTH):
        last = i == DEPTH - 1
        mod_l = silu_c @ ada_w[i] + ada_b[i]
        mod_c = silu_cc @ ada_w[i] + ada_b[i]
        sh1_l, s1_l, g1_l, sh2_l, s2_l, g2_l = jnp.split(mod_l[:, None, :], 6, axis=-1)
        sh1_c, s1_c, g1_c, sh2_c, s2_c, g2_c = jnp.split(mod_c, 6, axis=-1)

        hl = modulate(rms_norm(xl, norm_mix_g[i]), sh1_l, s1_l)
        hc = modulate(rms_norm(xc, norm_mix_g[i]), sh1_c, s1_c)
        u_l, q_l, k_l, v_l, gr_l, gs_l, gt_l = jnp.split(hl @ w_in[i], IN_CUTS, axis=-1)
        u_c, q_c, k_c, v_c, gr_c, gs_c, gt_c = jnp.split(hc @ w_in[i], IN_CUTS, axis=-1)

        s5o_c, s5o_l = s5_mixer(u_c, u_l, s5_lam_re[i], s5_lam_im[i], s5_log_step[i], s5_b_re[i], s5_b_im[i],
                                s5_c_re[i], s5_c_im[i], s5_d[i], s5_glu_w[i])
        reto_c, reto_l = retention_mixer(q_c, k_c, v_c, gr_c, q_l, k_l, v_l, gr_l, ret_decay_logit[i])

        def merge(s5o, reto, gs, gt):
            m = jax.nn.sigmoid(gs) * (s5o @ w_branch_s5[i]) + jax.nn.sigmoid(gt) * (reto @ w_branch_ret[i])
            return m @ w_out[i]

        xl = xl + g1_l * merge(s5o_l, reto_l, gs_l, gt_l)
        hl2 = modulate(rms_norm(xl, norm_ffn_g[i]), sh2_l, s2_l)

        if last:
            y = moe_ffn(hl2.reshape(-1, D), w_router, router_bias, moe_w_gate[i], moe_w_up[i], moe_w_down[i])
            xl = xl + g2_l * y.reshape(xl.shape)
        else:
            xc = xc + g1_c * merge(s5o_c, reto_c, gs_c, gt_c)
            hc2 = modulate(rms_norm(xc, norm_ffn_g[i]), sh2_c, s2_c)
            tokens = jnp.concatenate([hl2.reshape(-1, D), hc2.reshape(-1, D)], axis=0)
            y = moe_ffn(tokens, w_router, router_bias, moe_w_gate[i], moe_w_up[i], moe_w_down[i])
            n_l = bsz * n_lat
            xl = xl + g2_l * y[:n_l].reshape(xl.shape)
            xc = xc + g2_c * y[n_l:].reshape(xc.shape)
    return rms_norm(xl, final_norm_g)
```

```python
import functools
import math

import numpy as np
import jax
import jax.numpy as jnp
from jax import lax
from jax.experimental import pallas as pl
from jax.experimental.pallas import tpu as pltpu

F32 = jnp.float32
BF16 = jnp.bfloat16
HIGHEST = lax.Precision.HIGHEST

D = 2048
DEPTH = 4
GRID_W = 64
EPS = 1e-6
S5_W = D // 2
S5_H = 16
S5_G = S5_W // S5_H
S5_P = 64
HEADS = 8
DK = 128
RET_W = HEADS * DK
K_SCALE = DK ** -0.5
ROPE_BASE = 10000.0
N_EXP = 16
N_GRP = 4
EPG = N_EXP // N_GRP
D_EXP = D // 2
IN_SPLITS = (S5_W, RET_W, RET_W, RET_W, RET_W, D, D)

LANES = 128
SUBLANES = 8
MXU_DIM = 256
VMEM_LIMIT = 56 * 1024 * 1024
TM = 256
S5_TC = 64
S5_PAIRS = S5_G // 2
S5_STATE = S5_PAIRS * MXU_DIM
S5_PB = 4
RET_C = TM
MOE_BLOCK = 512
TD = 512


def _cparams(*sem):
    return pltpu.CompilerParams(dimension_semantics=sem, vmem_limit_bytes=VMEM_LIMIT)


def _resident(shape):
    nd = len(shape)
    return pl.BlockSpec(shape, lambda *_: (0,) * nd, pipeline_mode=pl.Buffered(1))


def _rms_mod(x, g, shift, scale):
    y = x * lax.rsqrt(jnp.mean(x * x, axis=-1, keepdims=True) + EPS) * g
    return y * (1.0 + scale) + shift


def _adaln_kernel(c_ref, w_ref, b_ref, o_ref):
    c = c_ref[...]
    sc = c * jax.nn.sigmoid(c)
    o_ref[0] = jnp.dot(sc, w_ref[0], preferred_element_type=F32, precision=HIGHEST) + b_ref[0]


def _adaln(c_rows, ada_w, ada_b):
    rows = c_rows.shape[0]
    depth, _, n6 = ada_w.shape
    tn = 1536
    return pl.pallas_call(
        _adaln_kernel,
        grid=(depth, n6 // tn),
        in_specs=[
            pl.BlockSpec((rows, D), lambda i, n: (0, 0)),
            pl.BlockSpec((1, D, tn), lambda i, n: (i, 0, n)),
            pl.BlockSpec((1, 1, tn), lambda i, n: (i, 0, n)),
        ],
        out_specs=pl.BlockSpec((1, rows, tn), lambda i, n: (i, 0, n)),
        out_shape=jax.ShapeDtypeStruct((depth, rows, n6), F32),
        compiler_params=_cparams("arbitrary", "arbitrary"),
        name="adaln",
    )(c_rows, ada_w, ada_b.reshape(depth, 1, n6))


def _mod_spec(k, nb, n_ctx_t):
    return pl.BlockSpec((None, None, 1, D), lambda b, j: (jnp.where(j < n_ctx_t, nb, b), k, 0, 0))


def _vec_spec():
    return pl.BlockSpec((1, D), lambda b, j: (0, 0))


def _tile_spec(width):
    return pl.BlockSpec((None, TM, width), lambda b, j: (b, j, 0))


def _prologue_kernel(ctx_ref, x_ref, g_ref, sh_ref, sc_ref, xo_ref, h_ref, *, n_ctx_t):
    j = pl.program_id(1)

    def emit(src_ref):
        x = src_ref[...]
        xo_ref[...] = x
        h_ref[...] = _rms_mod(x, g_ref[...], sh_ref[...], sc_ref[...]).astype(BF16)

    @pl.when(j < n_ctx_t)
    def _():
        emit(ctx_ref)

    @pl.when(j >= n_ctx_t)
    def _():
        emit(x_ref)


def _prologue(ctx, x, gain, mod4, n_ctx_t):
    nb, seq, _ = x.shape
    la = ctx.shape[1] + seq
    return pl.pallas_call(
        functools.partial(_prologue_kernel, n_ctx_t=n_ctx_t),
        grid=(nb, la // TM),
        in_specs=[
            pl.BlockSpec((None, TM, D), lambda b, j: (b, jnp.minimum(j, n_ctx_t - 1), 0)),
            pl.BlockSpec((None, TM, D), lambda b, j: (b, jnp.maximum(j - n_ctx_t, 0), 0)),
            _vec_spec(), _mod_spec(0, nb, n_ctx_t), _mod_spec(1, nb, n_ctx_t),
        ],
        out_specs=[_tile_spec(D), _tile_spec(D)],
        out_shape=[jax.ShapeDtypeStruct((nb, la, D), F32), jax.ShapeDtypeStruct((nb, la, D), BF16)],
        compiler_params=_cparams("arbitrary", "arbitrary"),
        name="prologue",
    )(ctx, x, gain, mod4, mod4)


def _inproj_kernel(h_ref, w_ref, *o_refs, widths, gate):
    h = h_ref[...]
    off = 0
    for o_ref, width in zip(o_refs, widths):
        for c0 in range(0, width, 1024):
            acc = jnp.dot(h, w_ref[:, off + c0:off + c0 + 1024], preferred_element_type=F32)
            if gate:
                acc = jax.nn.sigmoid(acc)
            o_ref[:, c0:c0 + 1024] = acc.astype(o_ref.dtype)
        off += width


def _inproj_mix(h, w):
    nb, la, _ = h.shape
    widths = IN_SPLITS[:5]
    tile = _tile_spec(1024)
    return pl.pallas_call(
        functools.partial(_inproj_kernel, widths=widths, gate=False),
        grid=(nb, la // TM),
        in_specs=[_tile_spec(D), _resident(w.shape)],
        out_specs=[pl.BlockSpec((TM, S5_W), lambda b, j: (j, b)), tile, tile, tile, tile],
        out_shape=[jax.ShapeDtypeStruct((la, nb * S5_W), BF16)]
        + [jax.ShapeDtypeStruct((nb, la, RET_W), BF16)] * 4,
        compiler_params=_cparams("arbitrary", "arbitrary"),
        name="inproj_mix",
    )(h, w)


def _inproj_gates(h, w):
    nb, la, _ = h.shape
    return pl.pallas_call(
        functools.partial(_inproj_kernel, widths=IN_SPLITS[5:], gate=True),
        grid=(nb, la // TM),
        in_specs=[_tile_spec(D), _resident(w.shape)],
        out_specs=[_tile_spec(D), _tile_spec(D)],
        out_shape=[jax.ShapeDtypeStruct((nb, la, D), BF16)] * 2,
        compiler_params=_cparams("arbitrary", "arbitrary"),
        name="inproj_gates",
    )(h, w)


def _s5_tables(lam_re, lam_im, log_step, b_re, b_im, c_re, c_im):
    delta = jnp.exp(log_step.astype(F32))[..., None]
    lre, lim = lam_re.astype(F32), lam_im.astype(F32)
    mag = jnp.exp(lre * delta)
    lbr, lbi = mag * jnp.cos(lim * delta), mag * jnp.sin(lim * delta)
    den = lre * lre + lim * lim
    nr, ni = lbr - 1.0, lbi
    fr = (nr * lre + ni * lim) / den
    fi = (ni * lre - nr * lim) / den
    bre, bim = b_re.astype(F32), b_im.astype(F32)
    bbr = fr[..., None] * bre - fi[..., None] * bim
    bbi = fr[..., None] * bim + fi[..., None] * bre
    eye2 = jnp.eye(2, dtype=F32)

    def pairs(t):
        return t.reshape(2, S5_PAIRS, 2, *t.shape[2:])

    bb = jnp.stack([pairs(bbr), pairs(bbi)], axis=2)
    wsmall = jnp.einsum("djrgph,gk->djghrkp", bb, eye2)
    wsmall = wsmall.reshape(2, S5_PAIRS, 2 * S5_H, MXU_DIM)
    npos = MXU_DIM // (2 * S5_H)
    pos = jnp.eye(npos, dtype=F32)[jnp.arange(S5_PAIRS) % npos]
    wb = jnp.einsum("djkn,jq->djqkn", wsmall, pos).reshape(2, S5_PAIRS, MXU_DIM, MXU_DIM)

    cc = jnp.stack([pairs(c_re.astype(F32)), -pairs(c_im.astype(F32))], axis=2)
    csmall = jnp.einsum("djrghp,gk->djrgpkh", cc, eye2)
    csmall = csmall.reshape(2, S5_PAIRS, MXU_DIM, 2 * S5_H)
    ncol = LANES // (2 * S5_H)
    cpos = jnp.eye(ncol, dtype=F32)[jnp.arange(S5_PAIRS) % ncol]
    wc = jnp.einsum("djnc,jq->djnqc", csmall, cpos).reshape(2, S5_PAIRS, MXU_DIM, LANES)

    lam = jnp.stack([pairs(lbr), pairs(lbi)], axis=2)
    lam = lam.reshape(2, 2 * S5_PAIRS, 1, LANES)
    lam = jnp.broadcast_to(lam, (2, 2 * S5_PAIRS, SUBLANES, LANES))
    return wb.astype(BF16), wc.astype(BF16), lam


def _s5_kernel(u_ref, wb_ref, wc_ref, lam_ref, y_ref, s_ref, h_ref):
    d = pl.program_id(0)

    @pl.when(pl.program_id(1) == 0)
    def _():
        h_ref[...] = jnp.zeros_like(h_ref)

    for j in range(S5_PAIRS):
        cb = (j * 2 * S5_H) // MXU_DIM
        s_ref[:, j * MXU_DIM:(j + 1) * MXU_DIM] = jnp.dot(
            u_ref[:, cb * MXU_DIM:(cb + 1) * MXU_DIM], wb_ref[j], preferred_element_type=F32)

    for jb in range(0, S5_PAIRS, S5_PB):
        js = range(jb, jb + S5_PB)
        lam = [(lam_ref[2 * j], lam_ref[2 * j + 1]) for j in js]
        init = tuple(h_ref[2 * j + r] for j in js for r in range(2))

        def step(t, carry, js=js, lam=lam):
            tt = jnp.where(d == 0, t, S5_TC - 1 - t)
            rows = pl.ds(pl.multiple_of(tt * SUBLANES, SUBLANES), SUBLANES)
            out = []
            for q, j in enumerate(js):
                hr, hi = carry[2 * q], carry[2 * q + 1]
                lr, li = lam[q]
                re_cols = slice(j * MXU_DIM, j * MXU_DIM + LANES)
                im_cols = slice(j * MXU_DIM + LANES, (j + 1) * MXU_DIM)
                nr = lr * hr - li * hi + s_ref[rows, re_cols]
                ni = lr * hi + li * hr + s_ref[rows, im_cols]
                s_ref[rows, re_cols] = nr
                s_ref[rows, im_cols] = ni
                out += [nr, ni]
            return tuple(out)

        fin = lax.fori_loop(0, S5_TC, step, init)
        for q, j in enumerate(js):
            h_ref[2 * j] = fin[2 * q]
            h_ref[2 * j + 1] = fin[2 * q + 1]

    per_block = LANES // (2 * S5_H)
    for m in range(S5_W // LANES):
        acc = None
        for q in range(per_block):
            j = m * per_block + q
            part = jnp.dot(s_ref[:, j * MXU_DIM:(j + 1) * MXU_DIM].astype(BF16), wc_ref[j],
                           preferred_element_type=F32)
            acc = part if acc is None else acc + part
        y_ref[:, m * LANES:(m + 1) * LANES] = acc.astype(y_ref.dtype)


def _s5_scan(u_rows, wb, wc, lam, n_ctx_rows):
    rows = u_rows.shape[0]
    rb = S5_TC * SUBLANES
    nch = rows // rb
    ncc = n_ctx_rows // rb

    def chunk(d, i):
        bwd = jnp.where(i < ncc, ncc - 1 - i, nch - 1 + ncc - i)
        return jnp.where(d == 0, i, bwd)

    return pl.pallas_call(
        _s5_kernel,
        grid=(2, nch),
        in_specs=[
            pl.BlockSpec((rb, S5_W), lambda d, i: (chunk(d, i), 0)),
            pl.BlockSpec((None, S5_PAIRS, MXU_DIM, MXU_DIM), lambda d, i: (d, 0, 0, 0)),
            pl.BlockSpec((None, S5_PAIRS, MXU_DIM, LANES), lambda d, i: (d, 0, 0, 0)),
            pl.BlockSpec((None, 2 * S5_PAIRS, SUBLANES, LANES), lambda d, i: (d, 0, 0, 0)),
        ],
        out_specs=pl.BlockSpec((None, rb, S5_W), lambda d, i: (d, chunk(d, i), 0)),
        out_shape=jax.ShapeDtypeStruct((2, rows, S5_W), BF16),
        scratch_shapes=[pltpu.VMEM((rb, S5_STATE), F32),
                        pltpu.VMEM((2 * S5_PAIRS, SUBLANES, LANES), F32)],
        compiler_params=_cparams("arbitrary", "arbitrary"),
        name="s5_scan",
    )(u_rows, wb, wc, lam)


def _rope_tables(n_ctx, seq):
    half = DK // 2
    nf = half // 2
    t = np.arange(seq)
    pos = np.stack([(t // GRID_W).astype(np.float32), (t % GRID_W).astype(np.float32)], axis=1)
    freqs = (ROPE_BASE ** (-np.arange(nf, dtype=np.float32) / nf)).astype(np.float32)
    ang = (pos[:, :, None] * freqs[None, None, :]).astype(np.float32)
    cos, sin = np.cos(ang).astype(np.float32), np.sin(ang).astype(np.float32)
    zero = np.zeros_like(sin)
    cos_t = np.concatenate([cos, cos], axis=2).reshape(seq, DK)
    sa_t = np.concatenate([-sin, zero], axis=2).reshape(seq, DK)
    sb_t = np.concatenate([zero, sin], axis=2).reshape(seq, DK)
    ident = np.ones((n_ctx, DK), np.float32)
    zc = np.zeros((n_ctx, DK), np.float32)
    return (jnp.asarray(np.concatenate([ident, cos_t])), jnp.asarray(np.concatenate([zc, sa_t])),
            jnp.asarray(np.concatenate([zc, sb_t])))


def _ret_tables(decay_logit):
    lg = jax.nn.log_sigmoid(decay_logit.astype(F32))
    idx = jnp.arange(RET_C, dtype=F32)
    diff = idx[:, None] - idx[None, :]
    lg3 = lg[:, :, None, None]
    fwd = jnp.where(diff >= 0, jnp.exp(lg3 * jnp.maximum(diff, 0.0)), 0.0)
    bwd = jnp.where(diff <= 0, jnp.exp(lg3 * jnp.maximum(-diff, 0.0)), 0.0)
    dmask = jnp.stack([fwd[0], bwd[1]])
    lg2 = lg[:, :, None]
    xi = jnp.stack([jnp.exp(lg2[0] * (idx + 1.0)), jnp.exp(lg2[1] * (RET_C - idx))])
    zeta = jnp.stack([jnp.exp(lg2[0] * (RET_C - 1.0 - idx)), jnp.exp(lg2[1] * idx)])
    cd = jnp.exp(lg * RET_C)
    wide = lambda t: jnp.broadcast_to(t[..., None], t.shape + (DK,))
    return dmask, wide(xi), wide(zeta), jnp.broadcast_to(cd[:, :, None, None], (2, HEADS, 1, DK))


def _rope(x, cos, sa, sb):
    nf = DK // 4
    return x * cos + pltpu.roll(x, DK - nf, 1) * sa + pltpu.roll(x, nf, 1) * sb


def _ret_kernel(*refs, backward):
    if backward:
        (q_ref, k_ref, v_ref, cos_ref, sa_ref, sb_ref, dm_ref, xi_ref, ze_ref, cd_ref,
         of_ref, g_ref, o_ref, s_ref) = refs
    else:
        (q_ref, k_ref, v_ref, cos_ref, sa_ref, sb_ref, dm_ref, xi_ref, ze_ref, cd_ref,
         o_ref, s_ref) = refs

    @pl.when(pl.program_id(1) == 0)
    def _():
        s_ref[...] = jnp.zeros_like(s_ref)

    cos, sa, sb = cos_ref[...], sa_ref[...], sb_ref[...]
    for h in range(HEADS):
        cols = slice(h * DK, (h + 1) * DK)
        qh = _rope(q_ref[:, cols].astype(F32), cos, sa, sb)
        kh = _rope(k_ref[:, cols].astype(F32), cos, sa, sb) * K_SCALE
        vh = v_ref[:, cols]
        qb = qh.astype(BF16)
        scores = lax.dot_general(qb, kh.astype(BF16), (((1,), (1,)), ((), ())), preferred_element_type=F32)
        p = (scores * dm_ref[h]).astype(BF16)
        state = s_ref[h]
        o = jnp.dot(p, vh, preferred_element_type=F32)
        o = o + jnp.dot(qb, state.astype(BF16), preferred_element_type=F32) * xi_ref[h]
        kz = (kh * ze_ref[h]).astype(BF16)
        s_ref[h] = state * cd_ref[h] + lax.dot_general(
            kz, vh, (((0,), (0,)), ((), ())), preferred_element_type=F32)
        if backward:
            tot = o + of_ref[:, cols]
            tot = tot * lax.rsqrt(jnp.mean(tot * tot, axis=-1, keepdims=True) + EPS)
            g = g_ref[:, cols].astype(F32)
            o_ref[:, cols] = (tot * (g * jax.nn.sigmoid(g))).astype(o_ref.dtype)
        else:
            o_ref[:, cols] = o


def _retention(q, k, v, rope, tables, direction, n_ctx_t, o_fwd=None, gate=None):
    nb, la, _ = q.shape
    nch = la // RET_C
    backward = direction == 1

    def chunk(j):
        if not backward:
            return j
        return jnp.where(j < n_ctx_t, n_ctx_t - 1 - j, nch - 1 + n_ctx_t - j)

    tile = pl.BlockSpec((None, RET_C, RET_W), lambda b, j: (b, chunk(j), 0))
    rope_spec = pl.BlockSpec((RET_C, DK), lambda b, j: (chunk(j), 0))
    dmask, xi, zeta, cd = tables

    def dir_spec(shape):
        nd = len(shape)
        return pl.BlockSpec((None,) + shape[1:], lambda b, j: (direction,) + (0,) * (nd - 1))

    args = [q, k, v, *rope, dmask, xi, zeta, cd]
    specs = [tile, tile, tile, rope_spec, rope_spec, rope_spec,
             dir_spec(dmask.shape), dir_spec(xi.shape), dir_spec(zeta.shape), dir_spec(cd.shape)]
    if backward:
        args += [o_fwd, gate]
        specs += [tile, tile]
    return pl.pallas_call(
        functools.partial(_ret_kernel, backward=backward),
        grid=(nb, nch),
        in_specs=specs,
        out_specs=tile,
        out_shape=jax.ShapeDtypeStruct((nb, la, RET_W), BF16 if backward else F32),
        scratch_shapes=[pltpu.VMEM((HEADS, DK, DK), F32)],
        compiler_params=_cparams("arbitrary", "arbitrary"),
        name="retention_bwd" if backward else "retention_fwd",
    )(*args)


def _merge_kernel(u_ref, yf_ref, yb_ref, d_ref, ret_ref, gs_ref, gt_ref, wglu_ref, wbs_ref, wbr_ref, m_ref):
    y = d_ref[...] * u_ref[...].astype(F32) + yf_ref[...].astype(F32) + yb_ref[...].astype(F32)
    act = jax.nn.gelu(y)
    z = jnp.dot(act.astype(BF16), wglu_ref[...], preferred_element_type=F32)
    s5o = (act * jax.nn.sigmoid(z)).astype(BF16)
    a = jnp.dot(s5o, wbs_ref[...], preferred_element_type=F32)
    r = jnp.dot(ret_ref[...], wbr_ref[...], preferred_element_type=F32)
    m_ref[...] = (gs_ref[...].astype(F32) * a + gt_ref[...].astype(F32) * r).astype(m_ref.dtype)


def _merge(u_tm, y_tm, d_skip, reto, gs, gt, w_glu, w_bs5, w_bret):
    nb, la, _ = reto.shape
    return pl.pallas_call(
        _merge_kernel,
        grid=(nb, la // TM),
        in_specs=[
            pl.BlockSpec((TM, S5_W), lambda b, j: (j, b)),
            pl.BlockSpec((None, TM, S5_W), lambda b, j: (0, j, b)),
            pl.BlockSpec((None, TM, S5_W), lambda b, j: (1, j, b)),
            pl.BlockSpec((1, S5_W), lambda b, j: (0, 0)),
            _tile_spec(RET_W), _tile_spec(D), _tile_spec(D),
            _resident(w_glu.shape), _resident(w_bs5.shape), _resident(w_bret.shape),
        ],
        out_specs=_tile_spec(D),
        out_shape=jax.ShapeDtypeStruct((nb, la, D), BF16),
        compiler_params=_cparams("arbitrary", "arbitrary"),
        name="merge",
    )(u_tm, y_tm, y_tm, d_skip, reto, gs, gt, w_glu, w_bs5, w_bret)


def _outproj_router_kernel(m_ref, x_ref, wout_ref, g1_ref, gn_ref, sh_ref, sc_ref, wrt_ref, rb_ref, tri_ref,
                           xo_ref, h2_ref, ri_ref, rw_ref, cnt_ref, carry_ref):
    first = jnp.logical_and(pl.program_id(0) == 0, pl.program_id(1) == 0)

    @pl.when(first)
    def _():
        carry_ref[...] = jnp.zeros_like(carry_ref)

    o = jnp.dot(m_ref[...], wout_ref[...], preferred_element_type=F32)
    xn = x_ref[...] + g1_ref[...] * o
    xo_ref[...] = xn
    h2 = _rms_mod(xn, gn_ref[...], sh_ref[...], sc_ref[...])
    h2_ref[...] = h2

    logits = lax.dot_general(wrt_ref[...], h2, (((1,), (1,)), ((), ())),
                             preferred_element_type=F32, precision=HIGHEST)
    scores = jax.nn.sigmoid(logits)
    sel = scores + rb_ref[...]
    srow = [sel[e:e + 1, :] for e in range(N_EXP)]
    wrow = [scores[e:e + 1, :] for e in range(N_EXP)]

    def top2_sum(a):
        best = None
        for i in range(EPG):
            for j in range(i + 1, EPG):
                s = a[i] + a[j]
                best = s if best is None else jnp.maximum(best, s)
        return best

    gscore = [top2_sum(srow[EPG * g:EPG * (g + 1)]) for g in range(N_GRP)]
    gidx = jnp.zeros(gscore[0].shape, jnp.int32)
    best = gscore[0]
    for g in range(1, N_GRP):
        better = gscore[g] > best
        gidx = jnp.where(better, g, gidx)
        best = jnp.where(better, gscore[g], best)

    def pick(rows_, i):
        out = rows_[i]
        for g in range(1, N_GRP):
            out = jnp.where(gidx == g, rows_[EPG * g + i], out)
        return out

    cand = [pick(srow, i) for i in range(EPG)]
    cwt = [pick(wrow, i) for i in range(EPG)]
    rank = []
    for i in range(EPG):
        r = jnp.zeros(gidx.shape, jnp.int32)
        for j in range(EPG):
            if j == i:
                continue
            ahead = cand[j] > cand[i]
            if j < i:
                ahead = jnp.logical_or(ahead, cand[j] == cand[i])
            r = r + ahead.astype(jnp.int32)
        rank.append(r)

    def chosen(k):
        loc = jnp.zeros(gidx.shape, jnp.int32)
        wt = jnp.zeros(best.shape, F32)
        for i in range(EPG):
            hit = rank[i] == k
            loc = jnp.where(hit, i, loc)
            wt = jnp.where(hit, cwt[i], wt)
        return gidx * EPG + loc, wt

    e0, w0 = chosen(0)
    e1, w1 = chosen(1)
    tot = w0 + w1
    w0, w1 = w0 / tot, w1 / tot

    erow = lax.broadcasted_iota(jnp.int32, (N_EXP, TM), 0)
    hit0, hit1 = erow == e0, erow == e1
    onehot = jnp.where(jnp.logical_or(hit0, hit1), 1.0, 0.0)
    excl = jnp.dot(onehot.astype(BF16), tri_ref[...], preferred_element_type=F32)
    rk = carry_ref[...] + excl
    r0 = jnp.sum(jnp.where(hit0, rk, 0.0), axis=0, keepdims=True).astype(jnp.int32)
    r1 = jnp.sum(jnp.where(hit1, rk, 0.0), axis=0, keepdims=True).astype(jnp.int32)
    carry = carry_ref[...] + jnp.sum(onehot, axis=1, keepdims=True)
    carry_ref[...] = carry

    ri_ref[...] = jnp.zeros_like(ri_ref)
    rw_ref[...] = jnp.zeros_like(rw_ref)
    for row, val in enumerate((e0, e1, r0, r1)):
        ri_ref[row:row + 1, :] = val
    for row, val in enumerate((w0, w1)):
        rw_ref[row:row + 1, :] = val
    cnt_ref[...] = jnp.broadcast_to(carry, cnt_ref.shape)


def _outproj_router(m, x_all, w_out, mod4, gain, w_router_t, router_bias, tri, n_ctx_t):
    nb, la, _ = x_all.shape
    nt = la // TM
    tok = nb * la
    lane_tile = pl.BlockSpec((SUBLANES, TM), lambda b, j: (0, b * nt + j))
    return pl.pallas_call(
        _outproj_router_kernel,
        grid=(nb, nt),
        in_specs=[
            _tile_spec(D), _tile_spec(D), _resident(w_out.shape),
            _mod_spec(2, nb, n_ctx_t), _vec_spec(), _mod_spec(3, nb, n_ctx_t), _mod_spec(4, nb, n_ctx_t),
            pl.BlockSpec((N_EXP, D), lambda b, j: (0, 0)),
            pl.BlockSpec((N_EXP, 1), lambda b, j: (0, 0)),
            pl.BlockSpec((TM, TM), lambda b, j: (0, 0)),
        ],
        out_specs=[_tile_spec(D), _tile_spec(D), lane_tile, lane_tile,
                   pl.BlockSpec((N_EXP, LANES), lambda b, j: (0, 0))],
        out_shape=[jax.ShapeDtypeStruct((nb, la, D), F32), jax.ShapeDtypeStruct((nb, la, D), F32),
                   jax.ShapeDtypeStruct((SUBLANES, tok), jnp.int32),
                   jax.ShapeDtypeStruct((SUBLANES, tok), F32),
                   jax.ShapeDtypeStruct((N_EXP, LANES), F32)],
        scratch_shapes=[pltpu.VMEM((N_EXP, 1), F32)],
        compiler_params=_cparams("arbitrary", "arbitrary"),
        name="outproj_router",
    )(m, x_all, w_out, mod4, gain, mod4, mod4, w_router_t, router_bias, tri)


def _dispatch_kernel(dest_ref, h_ref, xs_in_ref, xs_ref, sem):
    del xs_in_ref

    def issue(r, carry):
        for k in range(2):
            pltpu.make_async_copy(h_ref.at[pl.ds(r, 1)], xs_ref.at[pl.ds(dest_ref[k, r], 1)], sem).start()
        return carry

    lax.fori_loop(0, TD, issue, 0)
    for k in range(2):
        pltpu.make_async_copy(h_ref, xs_ref.at[pl.ds(0, TD)], sem).wait()


def _dispatch(dest_tiles, h_rows, xs_zero):
    tok = h_rows.shape[0]
    return pl.pallas_call(
        _dispatch_kernel,
        grid=(tok // TD,),
        in_specs=[
            pl.BlockSpec((None, 2, TD), lambda i: (i, 0, 0), memory_space=pltpu.SMEM),
            pl.BlockSpec((TD, D), lambda i: (i, 0)),
            pl.BlockSpec(memory_space=pl.ANY),
        ],
        out_specs=pl.BlockSpec(memory_space=pl.ANY),
        out_shape=jax.ShapeDtypeStruct(xs_zero.shape, xs_zero.dtype),
        scratch_shapes=[pltpu.SemaphoreType.DMA(())],
        input_output_aliases={2: 0},
        compiler_params=_cparams("arbitrary"),
        name="moe_dispatch",
    )(dest_tiles, h_rows, xs_zero)


def _expert_kernel(be_ref, nu_ref, x_ref, wg_ref, wu_ref, wd_ref, y_ref):
    del be_ref
    used = pl.program_id(0) < nu_ref[0]

    @pl.when(used)
    def _():
        xb = x_ref[...].astype(BF16)
        gate = jnp.dot(xb, wg_ref[...], preferred_element_type=F32)
        up = jnp.dot(xb, wu_ref[...], preferred_element_type=F32)
        hid = (gate * jax.nn.sigmoid(gate) * up).astype(BF16)
        y_ref[...] = jnp.dot(hid, wd_ref[...], preferred_element_type=F32)

    @pl.when(jnp.logical_not(used))
    def _():
        y_ref[...] = jnp.zeros_like(y_ref)


def _experts(block_e, n_used, xs, w_gate, w_up, w_down):
    nblk = xs.shape[0] // MOE_BLOCK

    def row_block(i, be, nu):
        return (jnp.minimum(i, nu[0] - 1), 0)

    return pl.pallas_call(
        _expert_kernel,
        grid_spec=pltpu.PrefetchScalarGridSpec(
            num_scalar_prefetch=2,
            grid=(nblk,),
            in_specs=[
                pl.BlockSpec((MOE_BLOCK, D), row_block),
                pl.BlockSpec((None, D, D_EXP), lambda i, be, nu: (be[i], 0, 0)),
                pl.BlockSpec((None, D, D_EXP), lambda i, be, nu: (be[i], 0, 0)),
                pl.BlockSpec((None, D_EXP, D), lambda i, be, nu: (be[i], 0, 0)),
            ],
            out_specs=pl.BlockSpec((MOE_BLOCK, D), lambda i, be, nu: (i, 0)),
        ),
        out_shape=jax.ShapeDtypeStruct(xs.shape, F32),
        compiler_params=_cparams("arbitrary"),
        name="moe_experts",
    )(block_e, n_used, xs, w_gate, w_up, w_down)


def _combine_kernel(dest_ref, x_ref, ys_ref, wcol_ref, g2_ref, gn_ref, sh_ref, sc_ref, *rest, final, n_ctx_t):
    if final:
        out_ref, buf, sem = rest
    else:
        xo_ref, h_ref, buf, sem = rest

    def body():
        def issue(r, carry):
            for k in range(2):
                pltpu.make_async_copy(ys_ref.at[pl.ds(dest_ref[k, r], 1)], buf.at[k, pl.ds(r, 1)], sem).start()
            return carry

        lax.fori_loop(0, TM, issue, 0)
        for k in range(2):
            pltpu.make_async_copy(ys_ref.at[pl.ds(0, TM)], buf.at[k], sem).wait()
        wcol = wcol_ref[...]
        y = wcol[:, 0:1] * buf[0] + wcol[:, 1:2] * buf[1]
        xn = x_ref[...] + g2_ref[...] * y
        if final:
            out_ref[...] = xn * lax.rsqrt(jnp.mean(xn * xn, axis=-1, keepdims=True) + EPS) * gn_ref[...]
        else:
            xo_ref[...] = xn
            h_ref[...] = _rms_mod(xn, gn_ref[...], sh_ref[...], sc_ref[...]).astype(BF16)

    if final:
        pl.when(pl.program_id(1) >= n_ctx_t)(body)
    else:
        body()


def _combine(dest_tiles, x_mid, ys, wcol, mod4, gain, mod4_next, n_ctx_t, final):
    nb, la, _ = x_mid.shape
    nt = la // TM
    flat = lambda b, j: (b * nt + j, 0, 0)
    if final:
        out_specs = pl.BlockSpec((None, TM, D), lambda b, j: (b, jnp.maximum(j - n_ctx_t, 0), 0))
        out_shape = jax.ShapeDtypeStruct((nb, la - n_ctx_t * TM, D), F32)
    else:
        out_specs = [_tile_spec(D), _tile_spec(D)]
        out_shape = [jax.ShapeDtypeStruct((nb, la, D), F32), jax.ShapeDtypeStruct((nb, la, D), BF16)]
    return pl.pallas_call(
        functools.partial(_combine_kernel, final=final, n_ctx_t=n_ctx_t),
        grid=(nb, nt),
        in_specs=[
            pl.BlockSpec((None, 2, TM), flat, memory_space=pltpu.SMEM),
            _tile_spec(D),
            pl.BlockSpec(memory_space=pl.ANY),
            pl.BlockSpec((TM, SUBLANES), lambda b, j: (b * nt + j, 0)),
            _mod_spec(5, nb, n_ctx_t), _vec_spec(), _mod_spec(0, nb, n_ctx_t), _mod_spec(1, nb, n_ctx_t),
        ],
        out_specs=out_specs,
        out_shape=out_shape,
        scratch_shapes=[pltpu.VMEM((2, TM, D), F32), pltpu.SemaphoreType.DMA(())],
        compiler_params=_cparams("arbitrary", "arbitrary"),
        name="moe_combine",
    )(dest_tiles, x_mid, ys, wcol, mod4, gain, mod4_next, mod4_next)


def _routing_plan(ri, cnt, n_blocks):
    counts = cnt[:, 0].astype(jnp.int32)
    padded = (counts + MOE_BLOCK - 1) // MOE_BLOCK * MOE_BLOCK
    pends = jnp.cumsum(padded)
    pstarts = pends - padded
    dest = jnp.stack([pstarts[ri[0]] + ri[2], pstarts[ri[1]] + ri[3]])
    n_used = pends[-1] // MOE_BLOCK
    blk = jnp.arange(n_blocks, dtype=jnp.int32)
    block_e = jnp.searchsorted(pends, blk * MOE_BLOCK, side="right").astype(jnp.int32)
    block_e = jnp.minimum(block_e, N_EXP - 1)
    last = jnp.take(block_e, jnp.maximum(n_used - 1, 0))
    block_e = jnp.where(blk < n_used, block_e, last)
    return dest.astype(jnp.int32), block_e, n_used.astype(jnp.int32).reshape(1)


def kernel(x, c, ctx, c_ctx, ada_w, ada_b, norm_mix_g, norm_ffn_g, final_norm_g, w_in, s5_lam_re, s5_lam_im,
           s5_log_step, s5_b_re, s5_b_im, s5_c_re, s5_c_im, s5_d, s5_glu_w, ret_decay_logit, w_branch_s5,
           w_branch_ret, w_out, w_router, router_bias, moe_w_gate, moe_w_up, moe_w_down):
    nb, seq, _ = x.shape
    n_ctx = ctx.shape[1]
    la = n_ctx + seq
    tok = nb * la
    assert nb == SUBLANES and n_ctx % TM == 0 and seq % TM == 0 and tok % TD == 0
    assert (n_ctx * nb) % (S5_TC * SUBLANES) == 0 and seq % GRID_W == 0
    n_ctx_t = n_ctx // TM
    depth = ada_w.shape[0]

    rows = -(-(nb + 1) // SUBLANES) * SUBLANES
    c_rows = jnp.concatenate([c, c_ctx[None, :], jnp.zeros((rows - nb - 1, D), F32)], axis=0)
    mod = _adaln(c_rows, ada_w, ada_b).reshape(depth, rows, 6, 1, D)

    split = sum(IN_SPLITS[:5])
    w_mix = w_in[:, :, :split].astype(BF16)
    w_gates = w_in[:, :, split:].astype(BF16)
    w_glu = s5_glu_w.astype(BF16)
    w_bs5 = w_branch_s5.astype(BF16)
    w_bret = w_branch_ret.astype(BF16)
    w_o = w_out.astype(BF16)
    wg, wu, wd = moe_w_gate.astype(BF16), moe_w_up.astype(BF16), moe_w_down.astype(BF16)
    w_router_t = w_router.astype(F32).T
    rbias = router_bias.astype(F32).reshape(N_EXP, 1)
    tri = jnp.asarray(np.triu(np.ones((TM, TM), np.float32), k=1), BF16)
    rope = _rope_tables(n_ctx, seq)
    gains = lambda g: g.astype(F32).reshape(1, D)

    n_blocks = -(-(tok * 2) // MOE_BLOCK) + N_EXP
    xs_zero = jnp.zeros((n_blocks * MOE_BLOCK, D), F32)

    x_all, h = _prologue(ctx, x, gains(norm_mix_g[0]), mod[0], n_ctx_t)
    out = None
    for i in range(depth):
        u_tm, q, k, v, gr = _inproj_mix(h, w_mix[i])
        gs, gt = _inproj_gates(h, w_gates[i])

        wb, wc, lam = _s5_tables(s5_lam_re[i], s5_lam_im[i], s5_log_step[i], s5_b_re[i], s5_b_im[i],
                                 s5_c_re[i], s5_c_im[i])
        y_rows = _s5_scan(u_tm.reshape(la * nb, S5_W), wb, wc, lam, n_ctx * nb)
        y_tm = y_rows.reshape(2, la, nb * S5_W)

        tables = _ret_tables(ret_decay_logit[i])
        o_fwd = _retention(q, k, v, rope, tables, 0, n_ctx_t)
        reto = _retention(q, k, v, rope, tables, 1, n_ctx_t, o_fwd, gr)

        m = _merge(u_tm, y_tm, s5_d[i].astype(F32).reshape(1, S5_W), reto, gs, gt, w_glu[i], w_bs5[i], w_bret[i])
        x_mid, h2, ri, rw, cnt = _outproj_router(m, x_all, w_o[i], mod[i], gains(norm_ffn_g[i]), w_router_t,
                                                 rbias, tri, n_ctx_t)

        dest, block_e, n_used = _routing_plan(ri, cnt, n_blocks)
        xs = _dispatch(dest.reshape(2, tok // TD, TD).transpose(1, 0, 2), h2.reshape(tok, D), xs_zero)
        ys = _experts(block_e, n_used, xs, wg[i], wu[i], wd[i])
        dest_t = dest.reshape(2, tok // TM, TM).transpose(1, 0, 2)
        wcol = rw.T
        if i + 1 < depth:
            x_all, h = _combine(dest_t, x_mid, ys, wcol, mod[i], gains(norm_mix_g[i + 1]), mod[i + 1],
                                n_ctx_t, final=False)
        else:
            out = _combine(dest_t, x_mid, ys, wcol, mod[i], gains(final_norm_g), mod[i], n_ctx_t, final=True)
    return out
```

```python
import functools

import numpy as np
import jax
import jax.numpy as jnp
from jax import lax
from jax.experimental import pallas as pl
from jax.experimental.pallas import tpu as pltpu

F32 = jnp.float32
BF16 = jnp.bfloat16
U32 = jnp.uint32
HIGHEST = lax.Precision.HIGHEST

D = 2048
GRID_W = 64
EPS = 1e-6
S5_W = D // 2
S5_H = 16
S5_G = S5_W // S5_H
S5_P = 64
HEADS = 8
DK = 128
RET_W = HEADS * DK
K_SCALE = DK ** -0.5
ROPE_BASE = 10000.0
N_EXP = 16
N_GRP = 4
EPG = N_EXP // N_GRP
D_EXP = D // 2
IN_SPLITS = (S5_W, RET_W, RET_W, RET_W, RET_W, D, D)

LANES = 128
SUBLANES = 8
MXU_DIM = 256
VMEM_LIMIT = 56 * 1024 * 1024
TM = 256
S5_TC = 64
S5_PAIRS = S5_G // 2
S5_STATE = S5_PAIRS * MXU_DIM
S5_PB = LANES // (2 * S5_H)
RET_C = TM
MOE_BLOCK = 512
TD = 512
HALF = D // 2


def _cparams(*sem):
    return pltpu.CompilerParams(dimension_semantics=sem, vmem_limit_bytes=VMEM_LIMIT)


def _resident(shape):
    nd = len(shape)
    return pl.BlockSpec(shape, lambda *_: (0,) * nd, pipeline_mode=pl.Buffered(1))


def _rms_mod(x, g, shift, scale):
    y = x * lax.rsqrt(jnp.mean(x * x, axis=-1, keepdims=True) + EPS) * g
    return y * (1.0 + scale) + shift


def _pack_rows(x):
    lo = lax.bitcast_convert_type(x[:, :HALF].astype(BF16).astype(F32), U32)
    hi = lax.bitcast_convert_type(x[:, HALF:].astype(BF16).astype(F32), U32)
    return (lo >> 16) | (hi & jnp.uint32(0xFFFF0000))


def _unpack_rows(w):
    lo = lax.bitcast_convert_type(w << 16, F32)
    hi = lax.bitcast_convert_type(w & jnp.uint32(0xFFFF0000), F32)
    return lo, hi


def _adaln_kernel(c_ref, w_ref, b_ref, o_ref):
    c = c_ref[...]
    sc = c * jax.nn.sigmoid(c)
    o_ref[0] = jnp.dot(sc, w_ref[0], preferred_element_type=F32, precision=HIGHEST) + b_ref[0]


def _adaln(c_rows, ada_w, ada_b):
    rows = c_rows.shape[0]
    depth, _, n6 = ada_w.shape
    tn = 1536
    return pl.pallas_call(
        _adaln_kernel,
        grid=(depth, n6 // tn),
        in_specs=[
            pl.BlockSpec((rows, D), lambda i, n: (0, 0)),
            pl.BlockSpec((1, D, tn), lambda i, n: (i, 0, n)),
            pl.BlockSpec((1, 1, tn), lambda i, n: (i, 0, n)),
        ],
        out_specs=pl.BlockSpec((1, rows, tn), lambda i, n: (i, 0, n)),
        out_shape=jax.ShapeDtypeStruct((depth, rows, n6), F32),
        compiler_params=_cparams("arbitrary", "arbitrary"),
        name="adaln",
    )(c_rows, ada_w, ada_b.reshape(depth, 1, n6))


def _mod_spec(k, nb, n_ctx_t):
    return pl.BlockSpec((None, None, 1, D), lambda b, j: (jnp.where(j < n_ctx_t, nb, b), k, 0, 0))


def _vec_spec():
    return pl.BlockSpec((1, D), lambda b, j: (0, 0))


def _tile_spec(width):
    return pl.BlockSpec((None, TM, width), lambda b, j: (b, j, 0))


def _prologue_kernel(ctx_ref, x_ref, g_ref, sh_ref, sc_ref, xo_ref, h_ref, *, n_ctx_t):
    j = pl.program_id(1)

    def emit(src_ref):
        x = src_ref[...]
        xo_ref[...] = x
        h_ref[...] = _rms_mod(x, g_ref[...], sh_ref[...], sc_ref[...]).astype(BF16)

    @pl.when(j < n_ctx_t)
    def _():
        emit(ctx_ref)

    @pl.when(j >= n_ctx_t)
    def _():
        emit(x_ref)


def _prologue(ctx, x, gain, mod4, n_ctx_t):
    nb, seq, _ = x.shape
    la = ctx.shape[1] + seq
    return pl.pallas_call(
        functools.partial(_prologue_kernel, n_ctx_t=n_ctx_t),
        grid=(nb, la // TM),
        in_specs=[
            pl.BlockSpec((None, TM, D), lambda b, j: (b, jnp.minimum(j, n_ctx_t - 1), 0)),
            pl.BlockSpec((None, TM, D), lambda b, j: (b, jnp.maximum(j - n_ctx_t, 0), 0)),
            _vec_spec(), _mod_spec(0, nb, n_ctx_t), _mod_spec(1, nb, n_ctx_t),
        ],
        out_specs=[_tile_spec(D), _tile_spec(D)],
        out_shape=[jax.ShapeDtypeStruct((nb, la, D), F32), jax.ShapeDtypeStruct((nb, la, D), BF16)],
        compiler_params=_cparams("arbitrary", "arbitrary"),
        name="prologue",
    )(ctx, x, gain, mod4, mod4)


def _rope(x, cos, sa, sb):
    nf = DK // 4
    return x * cos + pltpu.roll(x, DK - nf, 1) * sa + pltpu.roll(x, nf, 1) * sb


def _inproj_mix_kernel(h_ref, w_ref, cos_ref, sa_ref, sb_ref, u_ref, q_ref, k_ref, v_ref, g_ref):
    h = h_ref[...]

    def proj(n):
        return jnp.dot(h, w_ref[:, n * RET_W:(n + 1) * RET_W], preferred_element_type=F32)

    u_ref[...] = proj(0)
    cos, sa, sb = cos_ref[...], sa_ref[...], sb_ref[...]
    for n, o_ref, scale in ((1, q_ref, None), (2, k_ref, K_SCALE)):
        acc = proj(n)
        for hd in range(HEADS):
            cols = slice(hd * DK, (hd + 1) * DK)
            r = _rope(acc[:, cols], cos, sa, sb)
            if scale is not None:
                r = r * scale
            o_ref[:, cols] = r.astype(BF16)
    v_ref[...] = proj(3).astype(BF16)
    g_ref[...] = proj(4).astype(BF16)


def _inproj_mix(h, w, rope):
    nb, la, _ = h.shape
    tile = _tile_spec(RET_W)
    rope_spec = pl.BlockSpec((TM, DK), lambda b, j: (j, 0))
    return pl.pallas_call(
        _inproj_mix_kernel,
        grid=(nb, la // TM),
        in_specs=[_tile_spec(D), _resident(w.shape), rope_spec, rope_spec, rope_spec],
        out_specs=[tile] * 5,
        out_shape=[jax.ShapeDtypeStruct((nb, la, S5_W), F32)]
        + [jax.ShapeDtypeStruct((nb, la, RET_W), BF16)] * 4,
        compiler_params=_cparams("arbitrary", "arbitrary"),
        name="inproj_mix",
    )(h, w, *rope)


def _inproj_gates_kernel(h_ref, w_ref, gs_ref, gt_ref):
    h = h_ref[...]
    for n, o_ref in enumerate((gs_ref, gt_ref)):
        for c0 in range(0, D, 1024):
            acc = jnp.dot(h, w_ref[:, n * D + c0:n * D + c0 + 1024], preferred_element_type=F32)
            o_ref[:, c0:c0 + 1024] = jax.nn.sigmoid(acc).astype(BF16)


def _inproj_gates(h, w):
    nb, la, _ = h.shape
    return pl.pallas_call(
        _inproj_gates_kernel,
        grid=(nb, la // TM),
        in_specs=[_tile_spec(D), _resident(w.shape)],
        out_specs=[_tile_spec(D), _tile_spec(D)],
        out_shape=[jax.ShapeDtypeStruct((nb, la, D), BF16)] * 2,
        compiler_params=_cparams("arbitrary", "arbitrary"),
        name="inproj_gates",
    )(h, w)


def _s5_tables(lam_re, lam_im, log_step, b_re, b_im, c_re, c_im):
    delta = jnp.exp(log_step.astype(F32))[..., None]
    lre, lim = lam_re.astype(F32), lam_im.astype(F32)
    mag = jnp.exp(lre * delta)
    lbr, lbi = mag * jnp.cos(lim * delta), mag * jnp.sin(lim * delta)
    den = lre * lre + lim * lim
    nr, ni = lbr - 1.0, lbi
    fr = (nr * lre + ni * lim) / den
    fi = (ni * lre - nr * lim) / den
    bre, bim = b_re.astype(F32), b_im.astype(F32)
    bbr = fr[..., None] * bre - fi[..., None] * bim
    bbi = fr[..., None] * bim + fi[..., None] * bre
    eye2 = jnp.eye(2, dtype=F32)

    def pairs(t):
        return t.reshape(2, S5_PAIRS, 2, *t.shape[2:])

    bb = jnp.stack([pairs(bbr), pairs(bbi)], axis=2)
    wsmall = jnp.einsum("djrgph,gk->djghrkp", bb, eye2)
    wsmall = wsmall.reshape(2, S5_PAIRS, 2 * S5_H, MXU_DIM)
    npos = MXU_DIM // (2 * S5_H)
    pos = jnp.eye(npos, dtype=F32)[jnp.arange(S5_PAIRS) % npos]
    wb = jnp.einsum("djkn,jq->djqkn", wsmall, pos).reshape(2, S5_PAIRS, MXU_DIM, MXU_DIM)

    cc = jnp.stack([pairs(c_re.astype(F32)), -pairs(c_im.astype(F32))], axis=2)
    csmall = jnp.einsum("djrghp,gk->djrgpkh", cc, eye2)
    csmall = csmall.reshape(2, S5_PAIRS, MXU_DIM, 2 * S5_H)
    cpos = jnp.eye(S5_PB, dtype=F32)[jnp.arange(S5_PAIRS) % S5_PB]
    wc = jnp.einsum("djnc,jq->djnqc", csmall, cpos).reshape(2, S5_PAIRS, MXU_DIM, LANES)

    lam = jnp.stack([pairs(lbr), pairs(lbi)], axis=2)
    lam = lam.reshape(2, 2 * S5_PAIRS, 1, LANES)
    lam = jnp.broadcast_to(lam, (2, 2 * S5_PAIRS, SUBLANES, LANES))
    return wb.astype(BF16), wc.astype(BF16), lam


def _s5_kernel(*refs, backward, nch, ncc):
    if backward:
        (u_hbm, wb_ref, wc_ref, lam_ref, yf_ref, d_ref, y_hbm,
         ubuf, ub_ref, s_ref, h_ref, insem, obuf, outsem) = refs
    else:
        u_hbm, wb_ref, wc_ref, lam_ref, y_ref, ubuf, ub_ref, s_ref, h_ref, insem = refs
    i = pl.program_id(0)
    nb = ubuf.shape[2]
    rb = S5_TC * nb

    def chunk(step):
        if not backward:
            return step
        return jnp.where(step < ncc, ncc - 1 - step, nch - 1 + ncc - step)

    def in_copies(step, slot):
        t0 = pl.multiple_of(chunk(step) * S5_TC, S5_TC)
        return [pltpu.make_async_copy(u_hbm.at[b, pl.ds(t0, S5_TC)], ubuf.at[slot, :, b], insem.at[slot])
                for b in range(nb)]

    def out_copies(step, slot):
        t0 = pl.multiple_of(chunk(step) * S5_TC, S5_TC)
        return [pltpu.make_async_copy(obuf.at[slot, :, b], y_hbm.at[b, pl.ds(t0, S5_TC)], outsem.at[slot])
                for b in range(nb)]

    slot = i % 2

    @pl.when(i == 0)
    def _():
        h_ref[...] = jnp.zeros_like(h_ref)
        for cp in in_copies(0, 0):
            cp.start()

    @pl.when(i + 1 < nch)
    def _():
        for cp in in_copies(i + 1, 1 - slot):
            cp.start()

    for cp in in_copies(i, slot):
        cp.wait()
    if backward:
        @pl.when(i >= 2)
        def _():
            for cp in out_copies(i - 2, slot):
                cp.wait()

    ub_ref[...] = ubuf[slot].reshape(rb, S5_W).astype(BF16)

    def expand(jb):
        for q in range(S5_PB):
            j = jb * S5_PB + q
            cb = (j * 2 * S5_H) // MXU_DIM
            s_ref[:, j * MXU_DIM:(j + 1) * MXU_DIM] = jnp.dot(
                ub_ref[:, cb * MXU_DIM:(cb + 1) * MXU_DIM], wb_ref[j], preferred_element_type=F32)

    def scan(jb):
        js = range(jb * S5_PB, (jb + 1) * S5_PB)
        lam = [(lam_ref[2 * j], lam_ref[2 * j + 1]) for j in js]
        state = [(h_ref[2 * j], h_ref[2 * j + 1]) for j in js]
        for t in range(S5_TC):
            tt = S5_TC - 1 - t if backward else t
            rows = slice(tt * nb, (tt + 1) * nb)
            for q, j in enumerate(js):
                hr, hi = state[q]
                lr, li = lam[q]
                re_cols = slice(j * MXU_DIM, j * MXU_DIM + LANES)
                im_cols = slice(j * MXU_DIM + LANES, (j + 1) * MXU_DIM)
                nr = lr * hr - li * hi + s_ref[rows, re_cols]
                ni = lr * hi + li * hr + s_ref[rows, im_cols]
                s_ref[rows, re_cols] = nr
                s_ref[rows, im_cols] = ni
                state[q] = (nr, ni)
        for q, j in enumerate(js):
            h_ref[2 * j], h_ref[2 * j + 1] = state[q]

    def project(jb):
        acc = None
        for q in range(S5_PB):
            j = jb * S5_PB + q
            part = jnp.dot(s_ref[:, j * MXU_DIM:(j + 1) * MXU_DIM].astype(BF16), wc_ref[j],
                           preferred_element_type=F32)
            acc = part if acc is None else acc + part
        cols = slice(jb * LANES, (jb + 1) * LANES)
        if backward:
            u_blk = ubuf[slot, :, :, cols].reshape(rb, LANES)
            tot = acc + yf_ref[:, cols] + d_ref[:, cols] * u_blk
            obuf[slot, :, :, cols] = tot.reshape(S5_TC, nb, LANES)
        else:
            y_ref[:, cols] = acc

    nblk = S5_PAIRS // S5_PB
    expand(0)
    for jb in range(nblk):
        if jb + 1 < nblk:
            expand(jb + 1)
        scan(jb)
        project(jb)

    if backward:
        for cp in out_copies(i, slot):
            cp.start()

        @pl.when(i == nch - 1)
        def _():
            for cp in out_copies(i, slot):
                cp.wait()
            if nch >= 2:
                for cp in out_copies(i - 1, 1 - slot):
                    cp.wait()


def _s5_scan(u, wb, wc, lam, n_ctx, direction, y_fwd=None, d_skip=None):
    nb, la, _ = u.shape
    assert nb == SUBLANES
    rb = S5_TC * nb
    nch = la // S5_TC
    ncc = n_ctx // S5_TC
    backward = direction == 1

    def chunk(i):
        if not backward:
            return i
        return jnp.where(i < ncc, ncc - 1 - i, nch - 1 + ncc - i)

    def dir_spec(shape):
        nd = len(shape)
        return pl.BlockSpec((None,) + shape[1:], lambda i: (direction,) + (0,) * (nd - 1),
                            pipeline_mode=pl.Buffered(1))

    rows_spec = pl.BlockSpec((rb, S5_W), lambda i: (chunk(i), 0))
    in_specs = [pl.BlockSpec(memory_space=pl.ANY), dir_spec(wb.shape), dir_spec(wc.shape), dir_spec(lam.shape)]
    args = [u, wb, wc, lam]
    scratch = [pltpu.VMEM((2, S5_TC, nb, S5_W), F32), pltpu.VMEM((rb, S5_W), BF16),
               pltpu.VMEM((rb, S5_STATE), F32), pltpu.VMEM((2 * S5_PAIRS, SUBLANES, LANES), F32),
               pltpu.SemaphoreType.DMA((2,))]
    if backward:
        in_specs += [rows_spec, pl.BlockSpec((1, S5_W), lambda i: (0, 0))]
        args += [y_fwd, d_skip]
        out_specs = pl.BlockSpec(memory_space=pl.ANY)
        out_shape = jax.ShapeDtypeStruct((nb, la, S5_W), F32)
        scratch += [pltpu.VMEM((2, S5_TC, nb, S5_W), F32), pltpu.SemaphoreType.DMA((2,))]
    else:
        out_specs = rows_spec
        out_shape = jax.ShapeDtypeStruct((la * nb, S5_W), F32)
    return pl.pallas_call(
        functools.partial(_s5_kernel, backward=backward, nch=nch, ncc=ncc),
        grid=(nch,),
        in_specs=in_specs,
        out_specs=out_specs,
        out_shape=out_shape,
        scratch_shapes=scratch,
        compiler_params=_cparams("arbitrary"),
        name="s5_bwd" if backward else "s5_fwd",
    )(*args)


def _rope_tables(n_ctx, seq):
    half = DK // 2
    nf = half // 2
    t = np.arange(seq)
    pos = np.stack([(t // GRID_W).astype(np.float32), (t % GRID_W).astype(np.float32)], axis=1)
    freqs = (ROPE_BASE ** (-np.arange(nf, dtype=np.float32) / nf)).astype(np.float32)
    ang = (pos[:, :, None] * freqs[None, None, :]).astype(np.float32)
    cos, sin = np.cos(ang).astype(np.float32), np.sin(ang).astype(np.float32)
    zero = np.zeros_like(sin)
    cos_t = np.concatenate([cos, cos], axis=2).reshape(seq, DK)
    sa_t = np.concatenate([-sin, zero], axis=2).reshape(seq, DK)
    sb_t = np.concatenate([zero, sin], axis=2).reshape(seq, DK)
    ident = np.ones((n_ctx, DK), np.float32)
    zc = np.zeros((n_ctx, DK), np.float32)
    return (jnp.asarray(np.concatenate([ident, cos_t])), jnp.asarray(np.concatenate([zc, sa_t])),
            jnp.asarray(np.concatenate([zc, sb_t])))


def _ret_tables(decay_logit):
    lg = jax.nn.log_sigmoid(decay_logit.astype(F32))
    idx = jnp.arange(RET_C, dtype=F32)
    diff = idx[:, None] - idx[None, :]
    lg3 = lg[:, :, None, None]
    fwd = jnp.where(diff >= 0, jnp.exp(lg3 * jnp.maximum(diff, 0.0)), 0.0)
    bwd = jnp.where(diff <= 0, jnp.exp(lg3 * jnp.maximum(-diff, 0.0)), 0.0)
    dmask = jnp.stack([fwd[0], bwd[1]])
    lg2 = lg[:, :, None]
    xi = jnp.stack([jnp.exp(lg2[0] * (idx + 1.0)), jnp.exp(lg2[1] * (RET_C - idx))])
    zeta = jnp.stack([jnp.exp(lg2[0] * (RET_C - 1.0 - idx)), jnp.exp(lg2[1] * idx)])
    cd = jnp.exp(lg * RET_C)
    wide = lambda t: jnp.broadcast_to(t[..., None], t.shape + (DK,))
    return dmask, wide(xi), wide(zeta), jnp.broadcast_to(cd[:, :, None, None], (2, HEADS, 1, DK))


def _ret_kernel(*refs, backward):
    if backward:
        q_ref, k_ref, v_ref, dm_ref, xi_ref, ze_ref, cd_ref, of_ref, g_ref, o_ref, s_ref = refs
    else:
        q_ref, k_ref, v_ref, dm_ref, xi_ref, ze_ref, cd_ref, o_ref, s_ref = refs

    @pl.when(pl.program_id(1) == 0)
    def _():
        s_ref[...] = jnp.zeros_like(s_ref)

    for h in range(HEADS):
        cols = slice(h * DK, (h + 1) * DK)
        qb, kb, vh = q_ref[:, cols], k_ref[:, cols], v_ref[:, cols]
        scores = lax.dot_general(qb, kb, (((1,), (1,)), ((), ())), preferred_element_type=F32)
        p = (scores * dm_ref[h]).astype(BF16)
        state = s_ref[h]
        o = jnp.dot(p, vh, preferred_element_type=F32)
        o = o + jnp.dot(qb, state.astype(BF16), preferred_element_type=F32) * xi_ref[h]
        kz = (kb.astype(F32) * ze_ref[h]).astype(BF16)
        s_ref[h] = state * cd_ref[h] + lax.dot_general(
            kz, vh, (((0,), (0,)), ((), ())), preferred_element_type=F32)
        if backward:
            tot = o + of_ref[:, cols]
            tot = tot * lax.rsqrt(jnp.mean(tot * tot, axis=-1, keepdims=True) + EPS)
            g = g_ref[:, cols].astype(F32)
            o_ref[:, cols] = (tot * (g * jax.nn.sigmoid(g))).astype(o_ref.dtype)
        else:
            o_ref[:, cols] = o


def _retention(q, k, v, tables, direction, n_ctx_t, o_fwd=None, gate=None):
    nb, la, _ = q.shape
    nch = la // RET_C
    backward = direction == 1

    def chunk(j):
        if not backward:
            return j
        return jnp.where(j < n_ctx_t, n_ctx_t - 1 - j, nch - 1 + n_ctx_t - j)

    tile = pl.BlockSpec((None, RET_C, RET_W), lambda b, j: (b, chunk(j), 0))
    dmask, xi, zeta, cd = tables

    def dir_spec(shape):
        nd = len(shape)
        return pl.BlockSpec((None,) + shape[1:], lambda b, j: (direction,) + (0,) * (nd - 1))

    args = [q, k, v, dmask, xi, zeta, cd]
    specs = [tile, tile, tile, dir_spec(dmask.shape), dir_spec(xi.shape), dir_spec(zeta.shape), dir_spec(cd.shape)]
    if backward:
        args += [o_fwd, gate]
        specs += [tile, tile]
    return pl.pallas_call(
        functools.partial(_ret_kernel, backward=backward),
        grid=(nb, nch),
        in_specs=specs,
        out_specs=tile,
        out_shape=jax.ShapeDtypeStruct((nb, la, RET_W), BF16 if backward else F32),
        scratch_shapes=[pltpu.VMEM((HEADS, DK, DK), F32)],
        compiler_params=_cparams("arbitrary", "arbitrary"),
        name="retention_bwd" if backward else "retention_fwd",
    )(*args)


def _merge_kernel(y_ref, ret_ref, gs_ref, gt_ref, wglu_ref, wbs_ref, wbr_ref, m_ref):
    act = jax.nn.gelu(y_ref[...])
    z = jnp.dot(act.astype(BF16), wglu_ref[...], preferred_element_type=F32)
    s5o = (act * jax.nn.sigmoid(z)).astype(BF16)
    a = jnp.dot(s5o, wbs_ref[...], preferred_element_type=F32)
    r = jnp.dot(ret_ref[...], wbr_ref[...], preferred_element_type=F32)
    m_ref[...] = (gs_ref[...].astype(F32) * a + gt_ref[...].astype(F32) * r).astype(m_ref.dtype)


def _merge(y_s5, reto, gs, gt, w_glu, w_bs5, w_bret):
    nb, la, _ = reto.shape
    return pl.pallas_call(
        _merge_kernel,
        grid=(nb, la // TM),
        in_specs=[
            _tile_spec(S5_W), _tile_spec(RET_W), _tile_spec(D), _tile_spec(D),
            _resident(w_glu.shape), _resident(w_bs5.shape), _resident(w_bret.shape),
        ],
        out_specs=_tile_spec(D),
        out_shape=jax.ShapeDtypeStruct((nb, la, D), BF16),
        compiler_params=_cparams("arbitrary", "arbitrary"),
        name="merge",
    )(y_s5, reto, gs, gt, w_glu, w_bs5, w_bret)


def _outproj_kernel(m_ref, x_ref, wout_ref, g1_ref, gn_ref, sh_ref, sc_ref, wr_ref, xo_ref, hp_ref, lg_ref):
    o = jnp.dot(m_ref[...], wout_ref[...], preferred_element_type=F32)
    xn = x_ref[...] + g1_ref[...] * o
    xo_ref[...] = xn
    h2 = _rms_mod(xn, gn_ref[...], sh_ref[...], sc_ref[...])
    hp_ref[...] = _pack_rows(h2)
    hi = h2.astype(BF16)
    lo = (h2 - hi.astype(F32)).astype(BF16)
    lg_ref[...] = (jnp.dot(hi, wr_ref[...], preferred_element_type=F32)
                   + jnp.dot(lo, wr_ref[...], preferred_element_type=F32))


def _router_weight(w_router):
    w = w_router.astype(F32)
    hi = w.astype(BF16)
    lo = (w - hi.astype(F32)).astype(BF16)
    return jnp.concatenate([hi, lo, jnp.zeros((D, LANES - 2 * N_EXP), BF16)], axis=1)


def _outproj(m, x_all, w_out, mod4, gain, w_router2, n_ctx_t):
    nb, la, _ = x_all.shape
    nt = la // TM
    return pl.pallas_call(
        _outproj_kernel,
        grid=(nb, nt),
        in_specs=[
            _tile_spec(D), _tile_spec(D), _resident(w_out.shape),
            _mod_spec(2, nb, n_ctx_t), _vec_spec(), _mod_spec(3, nb, n_ctx_t), _mod_spec(4, nb, n_ctx_t),
            pl.BlockSpec((D, LANES), lambda b, j: (0, 0)),
        ],
        out_specs=[_tile_spec(D), _tile_spec(HALF), _tile_spec(LANES)],
        out_shape=[jax.ShapeDtypeStruct((nb, la, D), F32), jax.ShapeDtypeStruct((nb, la, HALF), U32),
                   jax.ShapeDtypeStruct((nb, la, LANES), F32)],
        compiler_params=_cparams("arbitrary", "arbitrary"),
        name="outproj",
    )(m, x_all, w_out, mod4, gain, mod4, mod4, w_router2)


def _router_kernel(lg_ref, rb_ref, tri_ref, ri_ref, rw_ref, cnt_ref, carry_ref):
    @pl.when(pl.program_id(0) == 0)
    def _():
        carry_ref[...] = jnp.zeros_like(carry_ref)

    width = lg_ref.shape[1]
    scores = jax.nn.sigmoid(lg_ref[...])
    sel = scores + rb_ref[...]
    srow = [sel[e:e + 1, :] for e in range(N_EXP)]
    wrow = [scores[e:e + 1, :] for e in range(N_EXP)]

    def top2_sum(a):
        best = None
        for i in range(EPG):
            for j in range(i + 1, EPG):
                s = a[i] + a[j]
                best = s if best is None else jnp.maximum(best, s)
        return best

    gscore = [top2_sum(srow[EPG * g:EPG * (g + 1)]) for g in range(N_GRP)]
    gidx = jnp.zeros(gscore[0].shape, jnp.int32)
    best = gscore[0]
    for g in range(1, N_GRP):
        better = gscore[g] > best
        gidx = jnp.where(better, g, gidx)
        best = jnp.where(better, gscore[g], best)

    def pick(rows_, i):
        out = rows_[i]
        for g in range(1, N_GRP):
            out = jnp.where(gidx == g, rows_[EPG * g + i], out)
        return out

    cand = [pick(srow, i) for i in range(EPG)]
    cwt = [pick(wrow, i) for i in range(EPG)]
    rank = []
    for i in range(EPG):
        r = jnp.zeros(gidx.shape, jnp.int32)
        for j in range(EPG):
            if j == i:
                continue
            ahead = cand[j] > cand[i]
            if j < i:
                ahead = jnp.logical_or(ahead, cand[j] == cand[i])
            r = r + ahead.astype(jnp.int32)
        rank.append(r)

    def chosen(k):
        loc = jnp.zeros(gidx.shape, jnp.int32)
        wt = jnp.zeros(best.shape, F32)
        for i in range(EPG):
            hit = rank[i] == k
            loc = jnp.where(hit, i, loc)
            wt = jnp.where(hit, cwt[i], wt)
        return gidx * EPG + loc, wt

    e0, w0 = chosen(0)
    e1, w1 = chosen(1)
    tot = w0 + w1
    ri_ref[...] = jnp.zeros_like(ri_ref)
    rw_ref[...] = jnp.zeros_like(rw_ref)
    ri_ref[0:1, :] = e0
    ri_ref[1:2, :] = e1
    rw_ref[0:1, :] = w0 / tot
    rw_ref[1:2, :] = w1 / tot

    erow = lax.broadcasted_iota(jnp.int32, (N_EXP, TM), 0)
    carry = carry_ref[...]
    for kb in range(width // TM):
        cols = slice(kb * TM, (kb + 1) * TM)
        hit0, hit1 = erow == e0[:, cols], erow == e1[:, cols]
        onehot = jnp.where(jnp.logical_or(hit0, hit1), 1.0, 0.0)
        rk = carry + jnp.dot(onehot.astype(BF16), tri_ref[...], preferred_element_type=F32)
        ri_ref[2:3, cols] = jnp.sum(jnp.where(hit0, rk, 0.0), axis=0, keepdims=True).astype(jnp.int32)
        ri_ref[3:4, cols] = jnp.sum(jnp.where(hit1, rk, 0.0), axis=0, keepdims=True).astype(jnp.int32)
        carry = carry + jnp.sum(onehot, axis=1, keepdims=True)
    carry_ref[...] = carry
    cnt_ref[...] = jnp.broadcast_to(carry, cnt_ref.shape)


def _router(logits, router_bias, tri, width):
    tok = logits.shape[1]
    lane_tile = pl.BlockSpec((SUBLANES, width), lambda i: (0, i))
    return pl.pallas_call(
        _router_kernel,
        grid=(tok // width,),
        in_specs=[
            pl.BlockSpec((N_EXP, width), lambda i: (0, i)),
            pl.BlockSpec((N_EXP, 1), lambda i: (0, 0)),
            pl.BlockSpec((TM, TM), lambda i: (0, 0)),
        ],
        out_specs=[lane_tile, lane_tile, pl.BlockSpec((N_EXP, LANES), lambda i: (0, 0))],
        out_shape=[jax.ShapeDtypeStruct((SUBLANES, tok), jnp.int32),
                   jax.ShapeDtypeStruct((SUBLANES, tok), F32),
                   jax.ShapeDtypeStruct((N_EXP, LANES), F32)],
        scratch_shapes=[pltpu.VMEM((N_EXP, 1), F32)],
        compiler_params=_cparams("arbitrary"),
        name="router",
    )(logits, router_bias, tri)


def _routing_plan(ri, cnt, n_blocks):
    counts = cnt[:, 0].astype(jnp.int32)
    padded = (counts + MOE_BLOCK - 1) // MOE_BLOCK * MOE_BLOCK
    pends = jnp.cumsum(padded)
    pstarts = pends - padded
    dest = jnp.stack([pstarts[ri[0]] + ri[2], pstarts[ri[1]] + ri[3]]).astype(jnp.int32)
    n_used = pends[-1] // MOE_BLOCK
    blk = jnp.arange(n_blocks, dtype=jnp.int32)
    block_e = jnp.sum((pends[None, :] <= blk[:, None] * MOE_BLOCK).astype(jnp.int32), axis=1)
    block_e = jnp.minimum(block_e, N_EXP - 1)
    last = jnp.sum(jnp.where(blk == n_used - 1, block_e, 0))
    block_e = jnp.where(blk < n_used, block_e, last)
    tail_blk = jnp.where(padded > 0, pends // MOE_BLOCK - 1, -1).astype(jnp.int32)
    return dest, block_e.astype(jnp.int32), n_used.astype(jnp.int32).reshape(1), tail_blk


def _dispatch_kernel(tail_ref, nu_ref, dest_ref, h_ref, xs_ref, zbuf, sem, *, n_blocks):
    def zero_block(blk):
        rows = pl.ds(pl.multiple_of(blk * MOE_BLOCK, MOE_BLOCK), MOE_BLOCK)
        return pltpu.make_async_copy(zbuf, xs_ref.at[rows], sem)

    @pl.when(pl.program_id(0) == 0)
    def _():
        zbuf[...] = jnp.zeros_like(zbuf)
        for e in range(N_EXP):
            @pl.when(tail_ref[e] >= 0)
            def _():
                cp = zero_block(tail_ref[e])
                cp.start()
                cp.wait()

        def unused(blk, carry):
            cp = zero_block(blk)
            cp.start()
            cp.wait()
            return carry

        lax.fori_loop(nu_ref[0], n_blocks, unused, 0)

    def issue(r, carry):
        for k in range(2):
            pltpu.make_async_copy(h_ref.at[pl.ds(r, 1)], xs_ref.at[pl.ds(dest_ref[k, r], 1)], sem).start()
        return carry

    lax.fori_loop(0, TD, issue, 0)
    for k in range(2):
        pltpu.make_async_copy(h_ref, xs_ref.at[pl.ds(0, TD)], sem).wait()


def _dispatch(tail_blk, n_used, dest_tiles, h_rows, n_blocks):
    tok = h_rows.shape[0]
    return pl.pallas_call(
        functools.partial(_dispatch_kernel, n_blocks=n_blocks),
        grid_spec=pltpu.PrefetchScalarGridSpec(
            num_scalar_prefetch=2,
            grid=(tok // TD,),
            in_specs=[
                pl.BlockSpec((None, 2, TD), lambda i, tl, nu: (i, 0, 0), memory_space=pltpu.SMEM),
                pl.BlockSpec((TD, HALF), lambda i, tl, nu: (i, 0)),
            ],
            out_specs=pl.BlockSpec(memory_space=pl.ANY),
            scratch_shapes=[pltpu.VMEM((MOE_BLOCK, HALF), U32), pltpu.SemaphoreType.DMA(())],
        ),
        out_shape=jax.ShapeDtypeStruct((n_blocks * MOE_BLOCK, HALF), U32),
        compiler_params=_cparams("arbitrary"),
        name="moe_dispatch",
    )(tail_blk, n_used, dest_tiles, h_rows)


def _expert_kernel(be_ref, nu_ref, x_ref, wg_ref, wu_ref, wd_ref, y_ref):
    del be_ref
    used = pl.program_id(0) < nu_ref[0]

    @pl.when(used)
    def _():
        lo, hi = _unpack_rows(x_ref[...])
        lo, hi = lo.astype(BF16), hi.astype(BF16)

        def proj(w_ref):
            return (jnp.dot(lo, w_ref[:HALF, :], preferred_element_type=F32)
                    + jnp.dot(hi, w_ref[HALF:, :], preferred_element_type=F32))

        gate = proj(wg_ref)
        hid = (gate * jax.nn.sigmoid(gate) * proj(wu_ref)).astype(BF16)
        y_ref[...] = _pack_rows(jnp.dot(hid, wd_ref[...], preferred_element_type=F32))

    @pl.when(jnp.logical_not(used))
    def _():
        y_ref[...] = jnp.zeros_like(y_ref)


def _experts(block_e, n_used, xs, w_gate, w_up, w_down):
    nblk = xs.shape[0] // MOE_BLOCK
    return pl.pallas_call(
        _expert_kernel,
        grid_spec=pltpu.PrefetchScalarGridSpec(
            num_scalar_prefetch=2,
            grid=(nblk,),
            in_specs=[
                pl.BlockSpec((MOE_BLOCK, HALF), lambda i, be, nu: (jnp.minimum(i, nu[0] - 1), 0)),
                pl.BlockSpec((None, D, D_EXP), lambda i, be, nu: (be[i], 0, 0)),
                pl.BlockSpec((None, D, D_EXP), lambda i, be, nu: (be[i], 0, 0)),
                pl.BlockSpec((None, D_EXP, D), lambda i, be, nu: (be[i], 0, 0)),
            ],
            out_specs=pl.BlockSpec((MOE_BLOCK, HALF), lambda i, be, nu: (i, 0)),
        ),
        out_shape=jax.ShapeDtypeStruct(xs.shape, U32),
        compiler_params=_cparams("arbitrary"),
        name="moe_experts",
    )(block_e, n_used, xs, w_gate, w_up, w_down)


def _combine_kernel(dest_ref, x_ref, ys_ref, wcol_ref, g2_ref, gn_ref, sh_ref, sc_ref, *rest, final, n_ctx_t):
    if final:
        out_ref, buf, sem = rest
    else:
        xo_ref, h_ref, buf, sem = rest

    def body():
        def issue(r, carry):
            for k in range(2):
                pltpu.make_async_copy(ys_ref.at[pl.ds(dest_ref[k, r], 1)], buf.at[k, pl.ds(r, 1)], sem).start()
            return carry

        lax.fori_loop(0, TM, issue, 0)
        for k in range(2):
            pltpu.make_async_copy(ys_ref.at[pl.ds(0, TM)], buf.at[k], sem).wait()
        wcol = wcol_ref[...]
        w0, w1 = wcol[:, 0:1], wcol[:, 1:2]
        lo0, hi0 = _unpack_rows(buf[0])
        lo1, hi1 = _unpack_rows(buf[1])
        y = jnp.concatenate([w0 * lo0 + w1 * lo1, w0 * hi0 + w1 * hi1], axis=-1)
        xn = x_ref[...] + g2_ref[...] * y
        if final:
            out_ref[...] = xn * lax.rsqrt(jnp.mean(xn * xn, axis=-1, keepdims=True) + EPS) * gn_ref[...]
        else:
            xo_ref[...] = xn
            h_ref[...] = _rms_mod(xn, gn_ref[...], sh_ref[...], sc_ref[...]).astype(BF16)

    if final:
        pl.when(pl.program_id(1) >= n_ctx_t)(body)
    else:
        body()


def _combine(dest_tiles, x_mid, ys, wcol, mod4, gain, mod4_next, n_ctx_t, final):
    nb, la, _ = x_mid.shape
    nt = la // TM
    flat = lambda b, j: (b * nt + j, 0, 0)
    if final:
        out_specs = pl.BlockSpec((None, TM, D), lambda b, j: (b, jnp.maximum(j - n_ctx_t, 0), 0))
        out_shape = jax.ShapeDtypeStruct((nb, la - n_ctx_t * TM, D), F32)
    else:
        out_specs = [_tile_spec(D), _tile_spec(D)]
        out_shape = [jax.ShapeDtypeStruct((nb, la, D), F32), jax.ShapeDtypeStruct((nb, la, D), BF16)]
    return pl.pallas_call(
        functools.partial(_combine_kernel, final=final, n_ctx_t=n_ctx_t),
        grid=(nb, nt),
        in_specs=[
            pl.BlockSpec((None, 2, TM), flat, memory_space=pltpu.SMEM),
            _tile_spec(D),
            pl.BlockSpec(memory_space=pl.ANY),
            pl.BlockSpec((TM, SUBLANES), lambda b, j: (b * nt + j, 0)),
            _mod_spec(5, nb, n_ctx_t), _vec_spec(), _mod_spec(0, nb, n_ctx_t), _mod_spec(1, nb, n_ctx_t),
        ],
        out_specs=out_specs,
        out_shape=out_shape,
        scratch_shapes=[pltpu.VMEM((2, TM, HALF), U32), pltpu.SemaphoreType.DMA(())],
        compiler_params=_cparams("arbitrary", "arbitrary"),
        name="moe_combine",
    )(dest_tiles, x_mid, ys, wcol, mod4, gain, mod4_next, mod4_next)


def kernel(x, c, ctx, c_ctx, ada_w, ada_b, norm_mix_g, norm_ffn_g, final_norm_g, w_in, s5_lam_re, s5_lam_im,
           s5_log_step, s5_b_re, s5_b_im, s5_c_re, s5_c_im, s5_d, s5_glu_w, ret_decay_logit, w_branch_s5,
           w_branch_ret, w_out, w_router, router_bias, moe_w_gate, moe_w_up, moe_w_down):
    nb, seq, _ = x.shape
    n_ctx = ctx.shape[1]
    la = n_ctx + seq
    tok = nb * la
    assert nb == SUBLANES and n_ctx % TM == 0 and seq % TM == 0 and tok % TD == 0
    assert n_ctx % S5_TC == 0 and seq % GRID_W == 0
    n_ctx_t = n_ctx // TM
    depth = ada_w.shape[0]

    rows = -(-(nb + 1) // SUBLANES) * SUBLANES
    c_rows = jnp.concatenate([c, c_ctx[None, :], jnp.zeros((rows - nb - 1, D), F32)], axis=0)
    mod = _adaln(c_rows, ada_w, ada_b).reshape(depth, rows, 6, 1, D)

    split = sum(IN_SPLITS[:5])
    w_mix = w_in[:, :, :split].astype(BF16)
    w_gates = w_in[:, :, split:].astype(BF16)
    w_glu = s5_glu_w.astype(BF16)
    w_bs5 = w_branch_s5.astype(BF16)
    w_bret = w_branch_ret.astype(BF16)
    w_o = w_out.astype(BF16)
    wg, wu, wd = moe_w_gate.astype(BF16), moe_w_up.astype(BF16), moe_w_down.astype(BF16)
    w_router2 = _router_weight(w_router)
    rbias = router_bias.astype(F32).reshape(N_EXP, 1)
    tri = jnp.asarray(np.triu(np.ones((TM, TM), np.float32), k=1), BF16)
    rope = _rope_tables(n_ctx, seq)
    gains = lambda g: g.astype(F32).reshape(1, D)
    n_blocks = -(-(tok * 2) // MOE_BLOCK) + N_EXP

    x_all, h = _prologue(ctx, x, gains(norm_mix_g[0]), mod[0], n_ctx_t)
    out = None
    for i in range(depth):
        u, q, k, v, gr = _inproj_mix(h, w_mix[i], rope)
        gs, gt = _inproj_gates(h, w_gates[i])

        wb, wc, lam = _s5_tables(s5_lam_re[i], s5_lam_im[i], s5_log_step[i], s5_b_re[i], s5_b_im[i],
                                 s5_c_re[i], s5_c_im[i])
        y_fwd = _s5_scan(u, wb, wc, lam, n_ctx, 0)
        y_s5 = _s5_scan(u, wb, wc, lam, n_ctx, 1, y_fwd, s5_d[i].astype(F32).reshape(1, S5_W))

        tables = _ret_tables(ret_decay_logit[i])
        o_fwd = _retention(q, k, v, tables, 0, n_ctx_t)
        reto = _retention(q, k, v, tables, 1, n_ctx_t, o_fwd, gr)

        m = _merge(y_s5, reto, gs, gt, w_glu[i], w_bs5[i], w_bret[i])
        x_mid, hp, lg = _outproj(m, x_all, w_o[i], mod[i], gains(norm_ffn_g[i]), w_router2, n_ctx_t)
        logits = (lg[:, :, :N_EXP] + lg[:, :, N_EXP:2 * N_EXP]).reshape(tok, N_EXP).T
        ri, rw, cnt = _router(logits, rbias, tri, la)

        dest, block_e, n_used, tail_blk = _routing_plan(ri, cnt, n_blocks)
        xs = _dispatch(tail_blk, n_used, dest.reshape(2, tok // TD, TD).transpose(1, 0, 2),
                       hp.reshape(tok, HALF), n_blocks)
        ys = _experts(block_e, n_used, xs, wg[i], wu[i], wd[i])
        dest_t = dest.reshape(2, tok // TM, TM).transpose(1, 0, 2)
        wcol = rw.T
        if i + 1 < depth:
            x_all, h = _combine(dest_t, x_mid, ys, wcol, mod[i], gains(norm_mix_g[i + 1]), mod[i + 1],
                                n_ctx_t, final=False)
        else:
            out = _combine(dest_t, x_mid, ys, wcol, mod[i], gains(final_norm_g), mod[i], n_ctx_t, final=True)
    return out
```
